```python
import math
import jax
import jax.numpy as jnp
from jax import lax
import numpy as np

D_MODEL = 1024
BATCH = 16
SEQ = 4096
DEPTH = 1
DEC_BATCH = 16
DEC_SEQ = 64
PAST_LEN = 1024

CHUNK = 64
RMS_EPS = 1e-6
RW_HEADS = 8
RW_HEAD_DIM = 64
RW_WIDTH = RW_HEADS * RW_HEAD_DIM
LORA_W = 64
LORA_A = 64
LORA_G = 128
RW_COLS = 3 * RW_WIDTH + LORA_W + LORA_A + LORA_G
GN_EPS = 64e-5
ATT_Q_HEADS = 8
ATT_KV_HEADS = 2
ATT_HEAD_DIM = 64
ATT_GROUP = ATT_Q_HEADS // ATT_KV_HEADS
ATT_Q_WIDTH = ATT_Q_HEADS * ATT_HEAD_DIM
ATT_KV_WIDTH = ATT_KV_HEADS * ATT_HEAD_DIM
ATT_COLS = ATT_Q_WIDTH + 2 * ATT_KV_WIDTH
WINDOW = 128
N_PREV_CHUNKS = WINDOW // CHUNK
N_BRANCH = 2
BRANCH_WIDTH = 512
GATE_COLS = N_BRANCH * D_MODEL
IN_COLS = RW_COLS + ATT_COLS + GATE_COLS
O_R = 0
O_K = O_R + RW_WIDTH
O_V = O_K + RW_WIDTH
O_W = O_V + RW_WIDTH
O_A = O_W + LORA_W
O_G = O_A + LORA_A
O_GATE = RW_COLS + ATT_COLS
PEER_HEADS = 8
PEER_NKEYS = 128
PEER_EXPERTS = PEER_NKEYS * PEER_NKEYS
PEER_KEY_DIM = 128
PEER_HALF = PEER_KEY_DIM // 2
PEER_TOPK = 16
PEER_BLOCK = 256

kernel_name = "rwkv7_swa_peer_streaming_step"


def rms_norm(x, g):
    xf = x.astype(jnp.float32)
    y = xf * lax.rsqrt(jnp.mean(xf * xf, axis=-1, keepdims=True) + RMS_EPS)
    return (y * g.astype(jnp.float32)).astype(x.dtype)


def project_inputs(h, h_prev, w_in, mu_shift):
    p = jnp.einsum('btd,dc->btc', h, w_in)
    p_rw = p[..., :RW_COLS]
    prev_row = jnp.einsum('bd,dc->bc', h_prev.astype(h.dtype), w_in[:, :RW_COLS])
    p_shift = jnp.concatenate([prev_row[:, None], p_rw[:, :-1]], axis=1)
    p_rw = p_rw + (p_shift - p_rw) * mu_shift
    return p_rw, p[..., RW_COLS:O_GATE], p[..., O_GATE:]


def rwkv7_branch(p_rw, wkv0, w0, w2, a0, a2, g2, k_k, k_a, r_k, lnx_w, lnx_b):
    B, T, _ = p_rw.shape
    f = p_rw.astype(jnp.float32)
    r = f[..., O_R:O_K]
    k = f[..., O_K:O_V]
    v = f[..., O_V:O_W]
    xw = f[..., O_W:O_A]
    xa = f[..., O_A:O_G]
    xg = f[..., O_G:RW_COLS]
    w = -jax.nn.softplus(-(w0 + jnp.tanh(xw) @ w2)) - 0.5
    decay = jnp.exp(-jnp.exp(w))
    a = jax.nn.sigmoid(a0 + xa @ a2)
    g = jax.nn.sigmoid(xg) @ g2
    heads = lambda t: t.reshape(B, T, RW_HEADS, RW_HEAD_DIM)
    kk = heads(k * k_k)
    kk = kk / jnp.maximum(jnp.sqrt(jnp.sum(kk * kk, axis=-1, keepdims=True)), 1e-12)
    k = k * (1.0 + (a - 1.0) * k_a)
    r_h, k_h, v_h, d_h, a_h = heads(r), heads(k), heads(v), heads(decay), heads(a)
    b_h = kk * a_h

    def step(S, inp):
        r_t, d_t, k_t, v_t, kk_t, b_t = inp
        sa = jnp.einsum('bhij,bhj->bhi', S, -kk_t)
        S = S * d_t[:, :, None, :] + sa[..., None] * b_t[:, :, None, :] + v_t[..., None] * k_t[:, :, None, :]
        y_t = jnp.einsum('bhij,bhj->bhi', S, r_t)
        return S, y_t

    xs = tuple(jnp.moveaxis(t, 1, 0) for t in (r_h, d_h, k_h, v_h, kk, b_h))
    S_fin, ys = lax.scan(step, wkv0.astype(jnp.float32), xs)
    y = jnp.moveaxis(ys, 0, 1)
    mean = jnp.mean(y, axis=-1, keepdims=True)
    var = jnp.mean(jnp.square(y - mean), axis=-1, keepdims=True)
    y = (y - mean) * lax.rsqrt(var + GN_EPS)
    y = y * lnx_w.astype(jnp.float32).reshape(RW_HEADS, RW_HEAD_DIM) + lnx_b.astype(jnp.float32).reshape(RW_HEADS, RW_HEAD_DIM)
    y = y + jnp.sum(r_h * k_h * r_k.astype(jnp.float32), axis=-1, keepdims=True) * v_h
    out = y.reshape(B, T, RW_WIDTH) * g
    return out.astype(p_rw.dtype), S_fin


def attn_core(q, k, v, q_pos, k_pos, k_valid, sinks):
    scale = ATT_HEAD_DIM ** -0.5
    s = jnp.einsum('bnqhgd,bnkhd->bnhgqk', q.astype(jnp.float32), k.astype(jnp.float32)) * scale
    slopes = jnp.exp2(-8.0 * jnp.arange(1, ATT_Q_HEADS + 1, dtype=jnp.float32) / ATT_Q_HEADS)
    slopes = slopes.reshape(ATT_KV_HEADS, ATT_GROUP)[None, None, :, :, None, None]
    dist = jnp.abs(q_pos[:, :, None] - k_pos[:, None, :]).astype(jnp.float32)
    s = s - slopes * dist[None, :, None, None, :, :]
    s = jnp.where(k_valid[None, :, None, None, None, :], s, -1e30)
    sink = sinks.astype(jnp.float32).reshape(ATT_KV_HEADS, ATT_GROUP)[None, None, :, :, None, None]
    m = jnp.maximum(jnp.max(s, axis=-1, keepdims=True), sink)
    p = jnp.exp(s - m)
    denom = jnp.sum(p, axis=-1, keepdims=True) + jnp.exp(sink - m)
    return jnp.einsum('bnhgqk,bnkhd->bnqhgd', p / denom, v.astype(jnp.float32))


def split_attn_cols(p_att):
    B, T, _ = p_att.shape
    q = p_att[..., :ATT_Q_WIDTH].reshape(B, T, ATT_KV_HEADS, ATT_GROUP, ATT_HEAD_DIM)
    k = p_att[..., ATT_Q_WIDTH:ATT_Q_WIDTH + ATT_KV_WIDTH].reshape(B, T, ATT_KV_HEADS, ATT_HEAD_DIM)
    v = p_att[..., ATT_Q_WIDTH + ATT_KV_WIDTH:].reshape(B, T, ATT_KV_HEADS, ATT_HEAD_DIM)
    return q, k, v


def attn_prompt(p_att, sinks):
    B, T, _ = p_att.shape
    nC = T // CHUNK
    q, k, v = split_attn_cols(p_att)
    qb = q.reshape(B, nC, CHUNK, ATT_KV_HEADS, ATT_GROUP, ATT_HEAD_DIM)
    pad = ((0, 0), (N_PREV_CHUNKS, 0), (0, 0), (0, 0), (0, 0))
    kp = jnp.pad(k.reshape(B, nC, CHUNK, ATT_KV_HEADS, ATT_HEAD_DIM), pad)
    vp = jnp.pad(v.reshape(B, nC, CHUNK, ATT_KV_HEADS, ATT_HEAD_DIM), pad)
    kband = jnp.concatenate([kp[:, j:j + nC] for j in range(N_PREV_CHUNKS + 1)], axis=2)
    vband = jnp.concatenate([vp[:, j:j + nC] for j in range(N_PREV_CHUNKS + 1)], axis=2)
    c = jnp.arange(nC)
    q_pos = c[:, None] * CHUNK + jnp.arange(CHUNK)[None, :]
    k_pos = (c[:, None] - N_PREV_CHUNKS) * CHUNK + jnp.arange((N_PREV_CHUNKS + 1) * CHUNK)[None, :]
    o = attn_core(qb, kband, vband, q_pos, k_pos, k_pos >= 0, sinks)
    keep = min(WINDOW, T)
    return o.reshape(B, T, ATT_Q_WIDTH).astype(p_att.dtype), k[:, T - keep:], v[:, T - keep:]


def attn_sample(p_att, cache_k, cache_v, sinks):
    B, S, _ = p_att.shape
    L = cache_k.shape[1]
    q, k, v = split_attn_cols(p_att)
    k_all = jnp.concatenate([cache_k.astype(k.dtype), k], axis=1)[:, None]
    v_all = jnp.concatenate([cache_v.astype(v.dtype), v], axis=1)[:, None]
    q_pos = (PAST_LEN + jnp.arange(S))[None, :]
    k_pos = (PAST_LEN - L + jnp.arange(L + S))[None, :]
    valid = jnp.ones((1, L + S), dtype=bool)
    o = attn_core(q[:, None], k_all, v_all, q_pos, k_pos, valid, sinks)
    return o.reshape(B, S, ATT_Q_WIDTH).astype(p_att.dtype), k, v


def merge_branches(o_rw, o_att, p_gate, w_branch, w_out):
    B, T, _ = p_gate.shape
    gates = jax.nn.sigmoid(p_gate.astype(jnp.float32)).astype(o_rw.dtype).reshape(B, T, N_BRANCH, D_MODEL)
    merged = gates[:, :, 0] * (o_rw @ w_branch[0]) + gates[:, :, 1] * (o_att @ w_branch[1])
    return merged @ w_out


def peer(h, w_q, sub_keys, u, v):
    B, T, D = h.shape
    n = B * T
    n_pad = (-n) % PEER_BLOCK
    blocks = jnp.pad(h.reshape(n, D), ((0, n_pad), (0, 0))).reshape(-1, PEER_BLOCK, D)

    def one_block(hb):
        q = (hb @ w_q).reshape(PEER_BLOCK, PEER_HEADS, 2, PEER_HALF).astype(jnp.float32)
        s = jnp.einsum('bhpk,pnk->bhpn', q, sub_keys.astype(jnp.float32))
        s1, i1 = lax.top_k(s[:, :, 0], PEER_TOPK)
        s2, i2 = lax.top_k(s[:, :, 1], PEER_TOPK)
        cand = (s1[..., :, None] + s2[..., None, :]).reshape(PEER_BLOCK, PEER_HEADS, PEER_TOPK * PEER_TOPK)
        cidx = (i1[..., :, None] * PEER_NKEYS + i2[..., None, :]).reshape(PEER_BLOCK, PEER_HEADS, PEER_TOPK * PEER_TOPK)
        best, pos = lax.top_k(cand, PEER_TOPK)
        idx = jnp.take_along_axis(cidx, pos, axis=-1)
        gate = jax.nn.softmax(best, axis=-1)
        ue = u[idx]
        ve = v[idx]
        act = jax.nn.gelu(jnp.einsum('bhkd,bd->bhk', ue, hb).astype(jnp.float32), approximate=False) * gate
        return jnp.einsum('bhk,bhkd->bd', act.astype(hb.dtype), ve)

    out = lax.map(one_block, blocks).reshape(-1, D)[:n]
    return out.reshape(B, T, D)


def trunk_layer(x, h_prev, wkv0, cache_k, cache_v, norm1_g, w_in, mu_shift, w_decay0, w_decay_lora,
                a_icl0, a_icl_lora, g_lora, k_k, k_a, r_k, lnx_w, lnx_b, attn_sinks, w_branch, w_out,
                norm2_g, peer_wq, peer_sub_keys, peer_u, peer_v):
    h = rms_norm(x, norm1_g)
    p_rw, p_att, p_gate = project_inputs(h, h_prev, w_in, mu_shift)
    o_rw, wkv_new = rwkv7_branch(p_rw, wkv0, w_decay0, w_decay_lora, a_icl0, a_icl_lora, g_lora,
                                 k_k, k_a, r_k, lnx_w, lnx_b)
    if cache_k is None:
        o_att, k_new, v_new = attn_prompt(p_att, attn_sinks)
    else:
        o_att, k_new, v_new = attn_sample(p_att, cache_k, cache_v, attn_sinks)
    x = x + merge_branches(o_rw, o_att, p_gate, w_branch, w_out)
    x = x + peer(rms_norm(x, norm2_g), peer_wq, peer_sub_keys, peer_u, peer_v)
    return x, h[:, -1], wkv_new.astype(x.dtype), k_new, v_new


def setup_inputs(seed: int = 0) -> dict:
    key = jax.random.key(seed)
    ks = iter(jax.random.split(key, 40))
    nrm = lambda shape, scale: jax.random.normal(next(ks), shape, jnp.float32) * scale
    att_cache = min(WINDOW, PAST_LEN)
    D = D_MODEL
    return {
        "x_prompt": nrm((BATCH, SEQ, D), 1.0),
        "x_sample": nrm((DEC_BATCH, DEC_SEQ, D), 1.0),
        "state_shift": nrm((DEPTH, DEC_BATCH, D), 1.0),
        "state_wkv": nrm((DEPTH, DEC_BATCH, RW_HEADS, RW_HEAD_DIM, RW_HEAD_DIM), 0.5),
        "cache_k": nrm((DEPTH, DEC_BATCH, att_cache, ATT_KV_HEADS, ATT_HEAD_DIM), 1.0),
        "cache_v": nrm((DEPTH, DEC_BATCH, att_cache, ATT_KV_HEADS, ATT_HEAD_DIM), 1.0),
        "norm1_g": 1.0 + nrm((DEPTH, D), 0.02),
        "w_in": nrm((DEPTH, D, IN_COLS), D ** -0.5),
        "mu_shift": jax.random.uniform(next(ks), (DEPTH, RW_COLS), jnp.float32),
        "w_decay0": jax.random.uniform(next(ks), (DEPTH, RW_WIDTH), jnp.float32, minval=-6.0, maxval=1.0),
        "w_decay_lora": nrm((DEPTH, LORA_W, RW_WIDTH), 0.1),
        "a_icl0": nrm((DEPTH, RW_WIDTH), 0.1),
        "a_icl_lora": nrm((DEPTH, LORA_A, RW_WIDTH), 0.1),
        "g_lora": nrm((DEPTH, LORA_G, RW_WIDTH), LORA_G ** -0.5),
        "k_k": 0.85 + nrm((DEPTH, RW_WIDTH), 0.02),
        "k_a": 1.0 + nrm((DEPTH, RW_WIDTH), 0.02),
        "r_k": nrm((DEPTH, RW_HEADS, RW_HEAD_DIM), 0.1),
        "lnx_w": 1.0 + nrm((DEPTH, RW_WIDTH), 0.02),
        "lnx_b": nrm((DEPTH, RW_WIDTH), 0.02),
        "attn_sinks": nrm((DEPTH, ATT_Q_HEADS), 0.5),
        "w_branch": nrm((DEPTH, N_BRANCH, BRANCH_WIDTH, D), BRANCH_WIDTH ** -0.5),
        "w_out": nrm((DEPTH, D, D), D ** -0.5),
        "norm2_g": 1.0 + nrm((DEPTH, D), 0.02),
        "peer_wq": nrm((DEPTH, D, PEER_HEADS * PEER_KEY_DIM), D ** -0.5),
        "peer_sub_keys": nrm((DEPTH, 2, PEER_NKEYS, PEER_HALF), PEER_HALF ** -0.5),
        "peer_u": nrm((DEPTH, PEER_EXPERTS, D), D ** -0.5),
        "peer_v": nrm((DEPTH, PEER_EXPERTS, D), (PEER_HEADS * PEER_TOPK) ** -0.5),
        "final_g": 1.0 + nrm((D,), 0.02),
    }


def reference(x_prompt, x_sample, state_shift, state_wkv, cache_k, cache_v, norm1_g, w_in, mu_shift,
              w_decay0, w_decay_lora, a_icl0, a_icl_lora, g_lora, k_k, k_a, r_k, lnx_w, lnx_b,
              attn_sinks, w_branch, w_out, norm2_g, peer_wq, peer_sub_keys, peer_u, peer_v, final_g):
    yp = x_prompt
    ys = x_sample
    shift_p, wkv_p, k_p, v_p = [], [], [], []
    shift_s, wkv_s, k_s, v_s = [], [], [], []
    for l in range(DEPTH):
        lp = (norm1_g[l], w_in[l], mu_shift[l], w_decay0[l], w_decay_lora[l], a_icl0[l], a_icl_lora[l],
              g_lora[l], k_k[l], k_a[l], r_k[l], lnx_w[l], lnx_b[l], attn_sinks[l], w_branch[l], w_out[l],
              norm2_g[l], peer_wq[l], peer_sub_keys[l], peer_u[l], peer_v[l])
        h0 = jnp.zeros((yp.shape[0], D_MODEL), yp.dtype)
        s0 = jnp.zeros((yp.shape[0], RW_HEADS, RW_HEAD_DIM, RW_HEAD_DIM), jnp.float32)
        yp, a1, a2, a3, a4 = trunk_layer(yp, h0, s0, None, None, *lp)
        shift_p.append(a1)
        wkv_p.append(a2)
        k_p.append(a3)
        v_p.append(a4)
        ys, b1, b2, b3, b4 = trunk_layer(ys, state_shift[l], state_wkv[l], cache_k[l], cache_v[l], *lp)
        shift_s.append(b1)
        wkv_s.append(b2)
        k_s.append(b3)
        v_s.append(b4)
    y_prompt = rms_norm(yp, final_g)
    y_sample = rms_norm(ys, final_g)
    return (y_prompt, y_sample, jnp.stack(shift_p), jnp.stack(wkv_p), jnp.stack(k_p), jnp.stack(v_p),
            jnp.stack(shift_s), jnp.stack(wkv_s), jnp.stack(k_s), jnp.stack(v_s))
```

```python
import functools
import math

import jax
import jax.numpy as jnp
from jax import lax
from jax.experimental import pallas as pl
from jax.experimental.pallas import tpu as pltpu

F32 = jnp.float32
BF16 = jnp.bfloat16
I32 = jnp.int32
U32 = jnp.uint32
HIGHEST = lax.Precision.HIGHEST

RMS_EPS = 1e-6
GN_EPS = 64e-5
CHUNK = 64
PAST_LEN = 1024
RW_HEADS = 8
RW_HEAD_DIM = 64
RW_WIDTH = RW_HEADS * RW_HEAD_DIM
LORA_W = 64
LORA_A = 64
LORA_G = 128
RW_COLS = 3 * RW_WIDTH + LORA_W + LORA_A + LORA_G
O_K = RW_WIDTH
O_V = 2 * RW_WIDTH
O_W = 3 * RW_WIDTH
O_A = O_W + LORA_W
O_G = O_A + LORA_A
ATT_Q_HEADS = 8
ATT_KV_HEADS = 2
ATT_HEAD_DIM = 64
ATT_GROUP = ATT_Q_HEADS // ATT_KV_HEADS
ATT_Q_WIDTH = ATT_Q_HEADS * ATT_HEAD_DIM
ATT_KV_WIDTH = ATT_KV_HEADS * ATT_HEAD_DIM
ATT_COLS = ATT_Q_WIDTH + 2 * ATT_KV_WIDTH
WINDOW = 128
N_PREV_CHUNKS = WINDOW // CHUNK
PEER_HEADS = 8
PEER_NKEYS = 128
PEER_HALF = 64
PEER_TOPK = 16
PEER_SEL = PEER_HEADS * PEER_TOPK

V7X_VMEM_BYTES = 64 * 1024 * 1024
VMEM_LIMIT = V7X_VMEM_BYTES * 3 // 4

RW_CHUNK = 64
RW_SUB = 16
GATHER_SLOTS = 4


def _cparams(sem):
    return pltpu.CompilerParams(dimension_semantics=sem, vmem_limit_bytes=VMEM_LIMIT)


def _row_tile(n, want):
    t = min(want, n)
    while n % t:
        t //= 2
    return t


def _rms(x, g):
    return x * lax.rsqrt(jnp.mean(x * x, axis=-1, keepdims=True) + RMS_EPS) * g


def _rms_rows_kernel(x_ref, g_ref, o_ref):
    o_ref[...] = _rms(x_ref[...], g_ref[...])


def rms_rows(x, g):
    return pl.pallas_call(
        _rms_rows_kernel,
        name="rms_rows",
        out_shape=jax.ShapeDtypeStruct(x.shape, F32),
    )(x, g.reshape(1, -1))


def _in_proj_kernel(x_ref, g_ref, w_ref, prw_ref, patt_ref, pgate_ref):
    h = _rms(x_ref[...], g_ref[...]).astype(BF16)
    o_att = RW_COLS
    o_gate = RW_COLS + ATT_COLS
    prw_ref[...] = jnp.dot(h, w_ref[:, :o_att], preferred_element_type=F32)
    patt_ref[...] = jnp.dot(h, w_ref[:, o_att:o_gate], preferred_element_type=F32)
    pgate_ref[...] = jnp.dot(h, w_ref[:, o_gate:], preferred_element_type=F32)


def in_proj(x2, g, w_bf):
    n, d = x2.shape
    cols = w_bf.shape[1]
    n_gate = cols - RW_COLS - ATT_COLS
    tm = _row_tile(n, 512)
    return pl.pallas_call(
        _in_proj_kernel,
        name="in_proj",
        grid=(n // tm,),
        in_specs=[
            pl.BlockSpec((tm, d), lambda i: (i, 0)),
            pl.BlockSpec((1, d), lambda i: (0, 0)),
            pl.BlockSpec((d, cols), lambda i: (0, 0)),
        ],
        out_specs=[
            pl.BlockSpec((tm, RW_COLS), lambda i: (i, 0)),
            pl.BlockSpec((tm, ATT_COLS), lambda i: (i, 0)),
            pl.BlockSpec((tm, n_gate), lambda i: (i, 0)),
        ],
        out_shape=[
            jax.ShapeDtypeStruct((n, RW_COLS), F32),
            jax.ShapeDtypeStruct((n, ATT_COLS), F32),
            jax.ShapeDtypeStruct((n, n_gate), F32),
        ],
        compiler_params=_cparams(("arbitrary",)),
    )(x2, g.reshape(1, d), w_bf)


def _shift_proj_kernel(h_ref, w_ref, o_ref):
    o_ref[...] = jnp.dot(h_ref[...].astype(BF16), w_ref[...], preferred_element_type=F32)


def shift_proj(h_prev, w_rw_bf):
    b = h_prev.shape[0]
    return pl.pallas_call(
        _shift_proj_kernel,
        name="shift_proj",
        out_shape=jax.ShapeDtypeStruct((b, w_rw_bf.shape[1]), F32),
        compiler_params=_cparams(None),
    )(h_prev, w_rw_bf)


def _dot_nt(a, b):
    return lax.dot_general(a, b, (((1,), (1,)), ((), ())), precision=HIGHEST, preferred_element_type=F32)


def _dot_tn(a, b):
    return lax.dot_general(a, b, (((0,), (0,)), ((), ())), precision=HIGHEST, preferred_element_type=F32)


def _dot_hi(a, b):
    return jnp.dot(a, b, precision=HIGHEST, preferred_element_type=F32)


def _unit_lower_inverse(lb, rowi, coli, c):
    eye = (rowi == coli).astype(F32)
    same_sub = (rowi // RW_SUB) == (coli // RW_SUB)
    nil = jnp.where(same_sub, -lb, 0.0)
    t = eye + nil
    p = nil
    span = 2
    while span < RW_SUB:
        p = _dot_hi(p, p)
        t = t + _dot_hi(t, p)
        span *= 2
    size = RW_SUB
    while size < c:
        pair = ((rowi // (2 * size)) == (coli // (2 * size))) & ((rowi // size) > (coli // size))
        off = jnp.where(pair, lb, 0.0)
        t = t - _dot_hi(t, _dot_hi(off, t))
        size *= 2
    return t


def _rwkv_kernel(prw_ref, prev_ref, wkv0_ref, mu_ref, w0_ref, w2_ref, a0_ref, a2_ref, g2_ref,
                 kk_ref, ka_ref, rk_ref, lw_ref, lb_ref, o_ref, sfin_ref, s_scr, carry_scr):
    c = prw_ref.shape[0]
    ci = pl.program_id(1)

    @pl.when(ci == 0)
    def _():
        s_scr[...] = wkv0_ref[...]
        carry_scr[...] = prev_ref[...]

    f = prw_ref[...]
    row = lax.broadcasted_iota(I32, f.shape, 0)
    prev = jnp.where(row == 0, carry_scr[...], pltpu.roll(f, 1, 0))
    carry_scr[...] = f[c - 1:c, :]
    fs = f + (prev - f) * mu_ref[...]

    xw = fs[:, O_W:O_A]
    xa = fs[:, O_A:O_G]
    xg = fs[:, O_G:RW_COLS]
    wpre = w0_ref[...] + jnp.dot(jnp.tanh(xw).astype(BF16), w2_ref[...], preferred_element_type=F32)
    w = -jax.nn.softplus(-wpre) - 0.5
    ld_all = -jnp.exp(w)
    a_all = jax.nn.sigmoid(a0_ref[...] + jnp.dot(xa.astype(BF16), a2_ref[...], preferred_element_type=F32))
    g_all = jnp.dot(jax.nn.sigmoid(xg).astype(BF16), g2_ref[...], preferred_element_type=F32)

    rowi = lax.broadcasted_iota(I32, (c, c), 0)
    coli = lax.broadcasted_iota(I32, (c, c), 1)
    strict = rowi > coli
    incl = rowi >= coli
    tri = incl.astype(F32)

    for h in range(RW_HEADS):
        sl = slice(h * RW_HEAD_DIM, (h + 1) * RW_HEAD_DIM)
        r = fs[:, sl]
        k = fs[:, O_K + h * RW_HEAD_DIM:O_K + (h + 1) * RW_HEAD_DIM]
        v = fs[:, O_V + h * RW_HEAD_DIM:O_V + (h + 1) * RW_HEAD_DIM]
        ld = ld_all[:, sl]
        a = a_all[:, sl]
        kk = k * kk_ref[:, sl]
        kk = kk / jnp.maximum(jnp.sqrt(jnp.sum(kk * kk, axis=-1, keepdims=True)), 1e-12)
        km = k * (1.0 + (a - 1.0) * ka_ref[:, sl])
        b = kk * a

        cum = _dot_hi(tri, ld)
        tot = cum[c - 1:c, :]
        d_inc = jnp.exp(cum)
        d_exc = jnp.exp(cum - ld)
        d_inv = jnp.exp(-cum)
        d_rem = jnp.exp(tot - cum)
        kkd = kk * d_exc
        rd = r * d_inc
        lhs = jnp.concatenate([kkd, rd], axis=0)
        gram = _dot_nt(lhs, jnp.concatenate([b * d_inv, km * d_inv], axis=0))
        l_b = jnp.where(strict, gram[:c, :c], 0.0)
        l_k = jnp.where(strict, gram[:c, c:], 0.0)
        a_b = jnp.where(incl, gram[c:, :c], 0.0)
        a_k = jnp.where(incl, gram[c:, c:], 0.0)
        t_inv = _unit_lower_inverse(l_b, rowi, coli, c)

        s0 = s_scr[h]
        ks = _dot_nt(lhs, s0)
        lav = _dot_hi(jnp.concatenate([l_k, a_k], axis=0), v)
        u = _dot_hi(t_inv, -ks[:c] - lav[:c])
        y = ks[c:] + lav[c:] + _dot_hi(a_b, u)
        s_new = s0 * jnp.exp(tot) + _dot_tn(jnp.concatenate([u, v], axis=0),
                                            jnp.concatenate([b * d_rem, km * d_rem], axis=0))
        s_scr[h] = s_new

        mean = jnp.mean(y, axis=-1, keepdims=True)
        var = jnp.mean(jnp.square(y - mean), axis=-1, keepdims=True)
        yn = (y - mean) * lax.rsqrt(var + GN_EPS)
        yn = yn * lw_ref[:, sl] + lb_ref[:, sl]
        yn = yn + jnp.sum(r * km * rk_ref[:, sl], axis=-1, keepdims=True) * v
        o_ref[:, sl] = (yn * g_all[:, sl]).astype(o_ref.dtype)

    @pl.when(ci == pl.num_programs(1) - 1)
    def _():
        sfin_ref[...] = s_scr[...]


def rwkv_branch(p_rw, prev_row, wkv0, mu, w0, w2, a0, a2, g2, k_k, k_a, r_k, lnx_w, lnx_b):
    bsz, t, _ = p_rw.shape
    c = RW_CHUNK
    assert t % c == 0
    row = lambda x: x.reshape(1, -1).astype(F32)
    vec = lambda n: pl.BlockSpec((1, n), lambda b, i: (0, 0))
    mat = lambda m, n: pl.BlockSpec((m, n), lambda b, i: (0, 0))
    state_spec = pl.BlockSpec((None, RW_HEADS, RW_HEAD_DIM, RW_HEAD_DIM), lambda b, i: (b, 0, 0, 0))
    return pl.pallas_call(
        _rwkv_kernel,
        name="rwkv",
        grid=(bsz, t // c),
        in_specs=[
            pl.BlockSpec((None, c, RW_COLS), lambda b, i: (b, i, 0)),
            pl.BlockSpec((None, 1, RW_COLS), lambda b, i: (b, 0, 0)),
            state_spec,
            vec(RW_COLS), vec(RW_WIDTH), mat(LORA_W, RW_WIDTH), vec(RW_WIDTH), mat(LORA_A, RW_WIDTH),
            mat(LORA_G, RW_WIDTH), vec(RW_WIDTH), vec(RW_WIDTH), vec(RW_WIDTH), vec(RW_WIDTH), vec(RW_WIDTH),
        ],
        out_specs=[
            pl.BlockSpec((None, c, RW_WIDTH), lambda b, i: (b, i, 0)),
            state_spec,
        ],
        out_shape=[
            jax.ShapeDtypeStruct((bsz, t, RW_WIDTH), BF16),
            jax.ShapeDtypeStruct((bsz, RW_HEADS, RW_HEAD_DIM, RW_HEAD_DIM), F32),
        ],
        scratch_shapes=[
            pltpu.VMEM((RW_HEADS, RW_HEAD_DIM, RW_HEAD_DIM), F32),
            pltpu.VMEM((1, RW_COLS), F32),
        ],
        compiler_params=_cparams(("arbitrary", "arbitrary")),
    )(p_rw, prev_row.reshape(bsz, 1, RW_COLS), wkv0.astype(F32), row(mu), row(w0), w2.astype(BF16), row(a0),
      a2.astype(BF16), g2.astype(BF16), row(k_k), row(k_a), row(r_k), row(lnx_w), row(lnx_b))


def _attn_kernel(sink_ref, q_ref, k0_ref, k1_ref, k2_ref, v0_ref, v1_ref, v2_ref, o_ref, *, first_valid):
    ci = pl.program_id(1)
    nk = (N_PREV_CHUNKS + 1) * CHUNK
    k3 = jnp.concatenate([k0_ref[...], k1_ref[...], k2_ref[...]], axis=0).astype(BF16)
    v3 = jnp.concatenate([v0_ref[...], v1_ref[...], v2_ref[...]], axis=0).astype(BF16)
    q = q_ref[...].astype(BF16)
    rows = ATT_GROUP * CHUNK
    qi = lax.broadcasted_iota(I32, (rows, nk), 0)
    kj = lax.broadcasted_iota(I32, (rows, nk), 1)
    dist = jnp.abs((qi % CHUNK) + N_PREV_CHUNKS * CHUNK - kj).astype(F32)
    valid = (kj // CHUNK + ci) >= first_valid
    scale = ATT_HEAD_DIM ** -0.5
    for g in range(ATT_KV_HEADS):
        kg = k3[:, g * ATT_HEAD_DIM:(g + 1) * ATT_HEAD_DIM]
        vg = v3[:, g * ATT_HEAD_DIM:(g + 1) * ATT_HEAD_DIM]
        heads = [g * ATT_GROUP + j for j in range(ATT_GROUP)]
        qg = jnp.concatenate([q[:, h * ATT_HEAD_DIM:(h + 1) * ATT_HEAD_DIM] for h in heads], axis=0)
        s = lax.dot_general(qg, kg, (((1,), (1,)), ((), ())), preferred_element_type=F32) * scale
        slope = jnp.concatenate(
            [jnp.full((CHUNK, 1), 2.0 ** (-8.0 * (h + 1) / ATT_Q_HEADS), F32) for h in heads], axis=0)
        sink = jnp.concatenate([jnp.full((CHUNK, 1), sink_ref[h], F32) for h in heads], axis=0)
        s = s - slope * dist
        s = jnp.where(valid, s, -1e30)
        m = jnp.maximum(jnp.max(s, axis=-1, keepdims=True), sink)
        p = jnp.exp(s - m)
        denom = jnp.sum(p, axis=-1, keepdims=True) + jnp.exp(sink - m)
        o = jnp.dot((p / denom).astype(BF16), vg, preferred_element_type=F32)
        for j, h in enumerate(heads):
            o_ref[:, h * ATT_HEAD_DIM:(h + 1) * ATT_HEAD_DIM] = o[j * CHUNK:(j + 1) * CHUNK].astype(o_ref.dtype)


def attn_branch(q, k_band, v_band, sinks, first_valid):
    bsz, t, _ = q.shape
    nc = t // CHUNK
    kv_spec = lambda j: pl.BlockSpec((None, CHUNK, ATT_KV_WIDTH), lambda b, c: (b, c + j, 0))
    return pl.pallas_call(
        functools.partial(_attn_kernel, first_valid=first_valid),
        name="attn",
        grid=(bsz, nc),
        in_specs=[
            pl.BlockSpec(memory_space=pltpu.SMEM),
            pl.BlockSpec((None, CHUNK, ATT_Q_WIDTH), lambda b, c: (b, c, 0)),
            kv_spec(0), kv_spec(1), kv_spec(2), kv_spec(0), kv_spec(1), kv_spec(2),
        ],
        out_specs=pl.BlockSpec((None, CHUNK, ATT_Q_WIDTH), lambda b, c: (b, c, 0)),
        out_shape=jax.ShapeDtypeStruct((bsz, t, ATT_Q_WIDTH), BF16),
        compiler_params=_cparams(("arbitrary", "arbitrary")),
    )(sinks.astype(F32), q, k_band, k_band, k_band, v_band, v_band, v_band)


def _top_rows(s, k):
    nrows = s.shape[0]
    row = lax.broadcasted_iota(I32, s.shape, 0)
    vals, idxs = [], []
    for _ in range(k):
        m = jnp.max(s, axis=0, keepdims=True)
        i = jnp.min(jnp.where(s == m, row, nrows), axis=0, keepdims=True)
        vals.append(m)
        idxs.append(i)
        s = jnp.where(row == i, -jnp.inf, s)
    return jnp.concatenate(vals, axis=0), jnp.concatenate(idxs, axis=0)


def _pick_rows(table, sel):
    r = table.shape[0]
    row = lax.broadcasted_iota(I32, table.shape, 0)
    out = []
    for j in range(sel.shape[0]):
        out.append(jnp.sum(jnp.where(row == sel[j:j + 1, :], table, 0), axis=0, keepdims=True))
    return jnp.concatenate(out, axis=0)


def _merge_kernel(x_ref, orw_ref, oatt_ref, pg_ref, wb_ref, wo_ref, g2_ref, wq_ref, sk_ref,
                  x1_ref, h2_ref, idx_ref, gate_ref, h2b_scr):
    d = x_ref.shape[1]

    @pl.when(pl.program_id(1) == 0)
    def _():
        gates = jax.nn.sigmoid(pg_ref[...])
        merged = (gates[:, :d] * jnp.dot(orw_ref[...], wb_ref[0], preferred_element_type=F32)
                  + gates[:, d:] * jnp.dot(oatt_ref[...], wb_ref[1], preferred_element_type=F32))
        x1 = x_ref[...] + jnp.dot(merged.astype(BF16), wo_ref[...], preferred_element_type=F32)
        x1_ref[...] = x1
        h2 = _rms(x1, g2_ref[...])
        h2_ref[...] = h2
        h2b_scr[...] = h2.astype(BF16)

    qv = jnp.dot(h2b_scr[...], wq_ref[...], preferred_element_type=F32).astype(BF16)
    top = []
    for p in range(2):
        s = lax.dot_general(sk_ref[p], qv[:, p * PEER_HALF:(p + 1) * PEER_HALF], (((1,), (1,)), ((), ())),
                            preferred_element_type=F32)
        top.append(_top_rows(s, PEER_TOPK))
    (s1, i1), (s2, i2) = top
    cand = jnp.concatenate([s1[i:i + 1, :] + s2 for i in range(PEER_TOPK)], axis=0)
    best, pos = _top_rows(cand, PEER_TOPK)
    e1 = _pick_rows(i1, pos // PEER_TOPK)
    e2 = _pick_rows(i2, pos % PEER_TOPK)
    ex = jnp.exp(best - best[0:1, :])
    idx_ref[...] = e1 * PEER_NKEYS + e2
    gate_ref[...] = ex / jnp.sum(ex, axis=0, keepdims=True)


def merge_and_route(x2, o_rw, o_att, p_gate, wb_bf, wo_bf, g2, wq_bf, sk_bf):
    n, d = x2.shape
    tm = _row_tile(n, 256)
    full = lambda shape: pl.BlockSpec(shape, lambda i, h: (0,) * len(shape))
    rows = lambda w: pl.BlockSpec((tm, w), lambda i, h: (i, 0))
    head_w = 2 * PEER_HALF
    return pl.pallas_call(
        _merge_kernel,
        name="merge_route",
        grid=(n // tm, PEER_HEADS),
        in_specs=[
            rows(d), rows(RW_WIDTH), rows(ATT_Q_WIDTH), rows(2 * d),
            full(wb_bf.shape), full(wo_bf.shape), full((1, d)),
            pl.BlockSpec((d, head_w), lambda i, h: (0, h)),
            full(sk_bf.shape),
        ],
        out_specs=[
            rows(d), rows(d),
            pl.BlockSpec((PEER_TOPK, tm), lambda i, h: (h, i)),
            pl.BlockSpec((PEER_TOPK, tm), lambda i, h: (h, i)),
        ],
        out_shape=[
            jax.ShapeDtypeStruct((n, d), F32),
            jax.ShapeDtypeStruct((n, d), F32),
            jax.ShapeDtypeStruct((PEER_SEL, n), I32),
            jax.ShapeDtypeStruct((PEER_SEL, n), F32),
        ],
        scratch_shapes=[pltpu.VMEM((tm, d), BF16)],
        compiler_params=_cparams(("arbitrary", "arbitrary")),
    )(x2, o_rw, o_att, p_gate, wb_bf, wo_bf, g2.reshape(1, d), wq_bf, sk_bf)


def _expert_kernel(idx_ref, gate_ref, h2_ref, x1_ref, fg_ref, tab_ref, y_ref, buf, sem):
    tb, d = h2_ref.shape
    lanes = 128
    n_groups = d // lanes

    def row_copy(n, m, slot):
        e = idx_ref[n, m]
        return pltpu.make_async_copy(tab_ref.at[pl.ds(e, 1), :], buf.at[slot, pl.ds(m, 1), :], sem.at[slot])

    def start_token(n):
        slot = n % GATHER_SLOTS
        for m in range(PEER_SEL):
            row_copy(n, m, slot).start()

    def wait_token(slot):
        pltpu.make_async_copy(tab_ref.at[pl.ds(0, PEER_SEL), :], buf.at[slot], sem.at[slot]).wait()

    for n in range(GATHER_SLOTS - 1):
        start_token(n)

    eye = lax.broadcasted_iota(I32, (PEER_SEL, PEER_SEL), 0) == lax.broadcasted_iota(I32, (PEER_SEL, PEER_SEL), 1)
    hi_mask = jnp.uint32(0xFFFF0000)

    def body(n, carry):
        @pl.when(n + GATHER_SLOTS - 1 < tb)
        def _():
            start_token(n + GATHER_SLOTS - 1)

        slot = n % GATHER_SLOTS
        wait_token(slot)
        hrow = h2_ref[pl.ds(n, 1), :]
        acc = jnp.zeros((PEER_SEL, lanes), F32)
        for c in range(n_groups):
            w = buf[slot, :, c * lanes:(c + 1) * lanes]
            u = lax.bitcast_convert_type(w & hi_mask, F32)
            acc = acc + u * hrow[:, c * lanes:(c + 1) * lanes]
        z = jnp.sum(acc, axis=-1, keepdims=True)
        gcol = jnp.sum(jnp.where(eye, gate_ref[pl.ds(n, 1), :], 0.0), axis=-1, keepdims=True)
        act = 0.5 * z * (1.0 + lax.erf(z * (2.0 ** -0.5))) * gcol
        outs = []
        for c in range(n_groups):
            w = buf[slot, :, c * lanes:(c + 1) * lanes]
            v = lax.bitcast_convert_type(w << 16, F32)
            outs.append(jnp.sum(v * act, axis=0, keepdims=True))
        x2 = x1_ref[pl.ds(n, 1), :] + jnp.concatenate(outs, axis=1)
        y_ref[pl.ds(n, 1), :] = _rms(x2, fg_ref[...])
        return carry

    lax.fori_loop(0, tb, body, 0)


def peer_experts(idx, gate, h2, x1, final_g, table):
    n, d = h2.shape
    tb = _row_tile(n, 128)
    rows = lambda w: pl.BlockSpec((tb, w), lambda i: (i, 0))
    return pl.pallas_call(
        _expert_kernel,
        name="experts",
        grid=(n // tb,),
        in_specs=[
            pl.BlockSpec((tb, PEER_SEL), lambda i: (i, 0), memory_space=pltpu.SMEM),
            rows(PEER_SEL), rows(d), rows(d),
            pl.BlockSpec((1, d), lambda i: (0, 0)),
            pl.BlockSpec(memory_space=pl.ANY),
        ],
        out_specs=rows(d),
        out_shape=jax.ShapeDtypeStruct((n, d), F32),
        scratch_shapes=[
            pltpu.VMEM((GATHER_SLOTS, PEER_SEL, d), U32),
            pltpu.SemaphoreType.DMA((GATHER_SLOTS,)),
        ],
        compiler_params=_cparams(("arbitrary",)),
    )(idx, gate, h2, x1, final_g.reshape(1, d), table)


def pack_expert_table(u, v):
    ub = lax.bitcast_convert_type(u.astype(BF16), jnp.uint16).astype(U32)
    vb = lax.bitcast_convert_type(v.astype(BF16), jnp.uint16).astype(U32)
    return (ub << 16) | vb


def _trunk(x, prev_row, wkv0, k_band_prefix, v_band_prefix, first_valid, lp):
    bsz, t, d = x.shape
    n = bsz * t
    x2 = x.reshape(n, d)
    p_rw, p_att, p_gate = in_proj(x2, lp["norm1_g"], lp["w_in"])
    h_last = rms_rows(x[:, -1, :], lp["norm1_g"])
    o_rw, wkv_new = rwkv_branch(p_rw.reshape(bsz, t, RW_COLS), prev_row, wkv0, lp["mu_shift"], lp["w_decay0"],
                                lp["w_decay_lora"], lp["a_icl0"], lp["a_icl_lora"], lp["g_lora"], lp["k_k"],
                                lp["k_a"], lp["r_k"], lp["lnx_w"], lp["lnx_b"])
    p_att3 = p_att.reshape(bsz, t, ATT_COLS)
    q = p_att3[..., :ATT_Q_WIDTH]
    k = p_att3[..., ATT_Q_WIDTH:ATT_Q_WIDTH + ATT_KV_WIDTH]
    v = p_att3[..., ATT_Q_WIDTH + ATT_KV_WIDTH:]
    k_band = jnp.concatenate([k_band_prefix, k], axis=1)
    v_band = jnp.concatenate([v_band_prefix, v], axis=1)
    o_att = attn_branch(q, k_band, v_band, lp["attn_sinks"], first_valid)
    x1, h2, idx_t, gate_t = merge_and_route(x2, o_rw.reshape(n, RW_WIDTH), o_att.reshape(n, ATT_Q_WIDTH), p_gate,
                                            lp["w_branch"], lp["w_out"], lp["norm2_g"], lp["peer_wq"],
                                            lp["peer_sub_keys"])
    y = peer_experts(idx_t.T, gate_t.T, h2, x1, lp["final_g"], lp["table"])
    kv_shape = (bsz, -1, ATT_KV_HEADS, ATT_HEAD_DIM)
    return y.reshape(bsz, t, d), h_last, wkv_new, k.reshape(kv_shape), v.reshape(kv_shape)


def kernel(x_prompt, x_sample, state_shift, state_wkv, cache_k, cache_v, norm1_g, w_in, mu_shift, w_decay0, w_decay_lora, a_icl0, a_icl_lora, g_lora, k_k, k_a, r_k, lnx_w, lnx_b, attn_sinks, w_branch, w_out, norm2_g, peer_wq, peer_sub_keys, peer_u, peer_v, final_g):
    assert w_in.shape[0] == 1, "single-layer trunk"
    lp = dict(
        norm1_g=norm1_g[0], w_in=w_in[0].astype(BF16), mu_shift=mu_shift[0], w_decay0=w_decay0[0],
        w_decay_lora=w_decay_lora[0], a_icl0=a_icl0[0], a_icl_lora=a_icl_lora[0], g_lora=g_lora[0],
        k_k=k_k[0], k_a=k_a[0], r_k=r_k[0].reshape(-1), lnx_w=lnx_w[0], lnx_b=lnx_b[0],
        attn_sinks=attn_sinks[0], w_branch=w_branch[0].astype(BF16), w_out=w_out[0].astype(BF16),
        norm2_g=norm2_g[0], peer_wq=peer_wq[0].astype(BF16), peer_sub_keys=peer_sub_keys[0].astype(BF16),
        final_g=final_g, table=pack_expert_table(peer_u[0], peer_v[0]),
    )
    bp, tp, _ = x_prompt.shape
    bs, ts, _ = x_sample.shape
    keep = min(WINDOW, tp)

    zeros_band = jnp.zeros((bp, WINDOW, ATT_KV_WIDTH), F32)
    yp, shift_p, wkv_p, kp, vp = _trunk(
        x_prompt, jnp.zeros((bp, RW_COLS), F32), jnp.zeros((bp, RW_HEADS, RW_HEAD_DIM, RW_HEAD_DIM), F32),
        zeros_band, zeros_band, N_PREV_CHUNKS, lp)

    prev_row = shift_proj(state_shift[0], lp["w_in"][:, :RW_COLS])
    ys, shift_s, wkv_s, ks, vs = _trunk(
        x_sample, prev_row, state_wkv[0], cache_k[0].reshape(bs, -1, ATT_KV_WIDTH),
        cache_v[0].reshape(bs, -1, ATT_KV_WIDTH), 0, lp)

    return (yp, ys, shift_p[None], wkv_p[None], kp[:, tp - keep:][None], vp[:, tp - keep:][None],
            shift_s[None], wkv_s[None], ks[None], vs[None])
```

```python
import functools
import math

import jax
import jax.numpy as jnp
from jax import lax
from jax.experimental import pallas as pl
from jax.experimental.pallas import tpu as pltpu

F32 = jnp.float32
BF16 = jnp.bfloat16
I32 = jnp.int32
U32 = jnp.uint32

RMS_EPS = 1e-6
GN_EPS = 64e-5
CHUNK = 64
PAST_LEN = 1024
RW_HEADS = 8
RW_HEAD_DIM = 64
RW_WIDTH = RW_HEADS * RW_HEAD_DIM
LORA_W = 64
LORA_A = 64
LORA_G = 128
RW_COLS = 3 * RW_WIDTH + LORA_W + LORA_A + LORA_G
O_K = RW_WIDTH
O_V = 2 * RW_WIDTH
O_W = 3 * RW_WIDTH
O_A = O_W + LORA_W
O_G = O_A + LORA_A
ATT_Q_HEADS = 8
ATT_KV_HEADS = 2
ATT_HEAD_DIM = 64
ATT_GROUP = ATT_Q_HEADS // ATT_KV_HEADS
ATT_Q_WIDTH = ATT_Q_HEADS * ATT_HEAD_DIM
ATT_KV_WIDTH = ATT_KV_HEADS * ATT_HEAD_DIM
ATT_COLS = ATT_Q_WIDTH + 2 * ATT_KV_WIDTH
WINDOW = 128
N_PREV_CHUNKS = WINDOW // CHUNK
PEER_HEADS = 8
PEER_NKEYS = 128
PEER_HALF = 64
PEER_TOPK = 16
PEER_SEL = PEER_HEADS * PEER_TOPK

V7X_VMEM_BYTES = 64 * 1024 * 1024
VMEM_LIMIT = V7X_VMEM_BYTES * 3 // 4

RW_CHUNK = 64
RW_SUB = 16
GATHER_SLOTS = 4


def _cparams(sem):
    return pltpu.CompilerParams(dimension_semantics=sem, vmem_limit_bytes=VMEM_LIMIT)


def _row_tile(n, want):
    t = min(want, n)
    while n % t:
        t //= 2
    return t


def _rms(x, g):
    return x * lax.rsqrt(jnp.mean(x * x, axis=-1, keepdims=True) + RMS_EPS) * g


def _rms_rows_kernel(x_ref, g_ref, o_ref):
    o_ref[...] = _rms(x_ref[...], g_ref[...])


def rms_rows(x, g):
    return pl.pallas_call(
        _rms_rows_kernel,
        name="rms_rows",
        out_shape=jax.ShapeDtypeStruct(x.shape, F32),
    )(x, g.reshape(1, -1))


def _in_proj_kernel(x_ref, g_ref, w_ref, prw_ref, patt_ref, pgate_ref):
    h = _rms(x_ref[...], g_ref[...]).astype(BF16)
    o_att = RW_COLS
    o_gate = RW_COLS + ATT_COLS
    prw_ref[...] = jnp.dot(h, w_ref[:, :o_att], preferred_element_type=F32)
    patt_ref[...] = jnp.dot(h, w_ref[:, o_att:o_gate], preferred_element_type=F32)
    pgate_ref[...] = jnp.dot(h, w_ref[:, o_gate:], preferred_element_type=F32)


def in_proj(x2, g, w_bf):
    n, d = x2.shape
    cols = w_bf.shape[1]
    n_gate = cols - RW_COLS - ATT_COLS
    tm = _row_tile(n, 512)
    return pl.pallas_call(
        _in_proj_kernel,
        name="in_proj",
        grid=(n // tm,),
        in_specs=[
            pl.BlockSpec((tm, d), lambda i: (i, 0)),
            pl.BlockSpec((1, d), lambda i: (0, 0)),
            pl.BlockSpec((d, cols), lambda i: (0, 0)),
        ],
        out_specs=[
            pl.BlockSpec((tm, RW_COLS), lambda i: (i, 0)),
            pl.BlockSpec((tm, ATT_COLS), lambda i: (i, 0)),
            pl.BlockSpec((tm, n_gate), lambda i: (i, 0)),
        ],
        out_shape=[
            jax.ShapeDtypeStruct((n, RW_COLS), F32),
            jax.ShapeDtypeStruct((n, ATT_COLS), F32),
            jax.ShapeDtypeStruct((n, n_gate), F32),
        ],
        compiler_params=_cparams(("arbitrary",)),
    )(x2, g.reshape(1, d), w_bf)


def _shift_proj_kernel(h_ref, w_ref, o_ref):
    o_ref[...] = jnp.dot(h_ref[...].astype(BF16), w_ref[...], preferred_element_type=F32)


def shift_proj(h_prev, w_rw_bf):
    b = h_prev.shape[0]
    return pl.pallas_call(
        _shift_proj_kernel,
        name="shift_proj",
        out_shape=jax.ShapeDtypeStruct((b, w_rw_bf.shape[1]), F32),
        compiler_params=_cparams(None),
    )(h_prev, w_rw_bf)


def _split(x):
    hi = x.astype(BF16)
    return hi, (x - hi.astype(F32)).astype(BF16)


def _dot3(a, b, ca, cb):
    a_hi, a_lo = _split(a)
    b_hi, b_lo = _split(b)
    a3 = jnp.concatenate([a_hi, a_hi, a_lo], axis=ca)
    b3 = jnp.concatenate([b_hi, b_lo, b_hi], axis=cb)
    return lax.dot_general(a3, b3, (((ca,), (cb,)), ((), ())), preferred_element_type=F32)


def _dot_nt(a, b):
    return _dot3(a, b, 1, 1)


def _dot_tn(a, b):
    return _dot3(a, b, 0, 0)


def _dot_hi(a, b):
    return _dot3(a, b, 1, 0)


def _unit_lower_inverse(lbs, rowi, coli, c):
    eye = (rowi == coli).astype(F32)
    same_sub = (rowi // RW_SUB) == (coli // RW_SUB)
    ps = [jnp.where(same_sub, -lb, 0.0) for lb in lbs]
    ts = [eye + p for p in ps]
    span = 2
    while span < RW_SUB:
        ps = [_dot_hi(p, p) for p in ps]
        ts = [t + _dot_hi(t, p) for t, p in zip(ts, ps)]
        span *= 2
    size = RW_SUB
    while size < c:
        pair = ((rowi // (2 * size)) == (coli // (2 * size))) & ((rowi // size) > (coli // size))
        mids = [_dot_hi(jnp.where(pair, lb, 0.0), t) for lb, t in zip(lbs, ts)]
        ts = [t - _dot_hi(t, m) for t, m in zip(ts, mids)]
        size *= 2
    return ts


def _rwkv_kernel(prw_ref, prev_ref, wkv0_ref, mu_ref, w0_ref, w2_ref, a0_ref, a2_ref, g2_ref,
                 kk_ref, ka_ref, rk_ref, lw_ref, lb_ref, o_ref, sfin_ref, s_scr, carry_scr):
    c = prw_ref.shape[0]
    ci = pl.program_id(1)

    @pl.when(ci == 0)
    def _():
        s_scr[...] = wkv0_ref[...]
        carry_scr[...] = prev_ref[...]

    f = prw_ref[...]
    row = lax.broadcasted_iota(I32, f.shape, 0)
    prev = jnp.where(row == 0, carry_scr[...], pltpu.roll(f, 1, 0))
    carry_scr[...] = f[c - 1:c, :]
    fs = f + (prev - f) * mu_ref[...]

    xw = fs[:, O_W:O_A]
    xa = fs[:, O_A:O_G]
    xg = fs[:, O_G:RW_COLS]
    wpre = w0_ref[...] + jnp.dot(jnp.tanh(xw).astype(BF16), w2_ref[...], preferred_element_type=F32)
    w = -jax.nn.softplus(-wpre) - 0.5
    ld_all = -jnp.exp(w)
    a_all = jax.nn.sigmoid(a0_ref[...] + jnp.dot(xa.astype(BF16), a2_ref[...], preferred_element_type=F32))
    g_all = jnp.dot(jax.nn.sigmoid(xg).astype(BF16), g2_ref[...], preferred_element_type=F32)

    rowi = lax.broadcasted_iota(I32, (c, c), 0)
    coli = lax.broadcasted_iota(I32, (c, c), 1)
    strict = rowi > coli
    incl = rowi >= coli
    n = RW_HEAD_DIM
    heads = range(RW_HEADS)

    r_all = fs[:, :O_K]
    k_all = fs[:, O_K:O_V]
    v_all = fs[:, O_V:O_W]
    cum = _dot_hi(incl.astype(F32), ld_all)
    tot = cum[c - 1:c, :]
    km_all = k_all * (1.0 + (a_all - 1.0) * ka_ref[...])
    rd_all = r_all * jnp.exp(cum)
    d_exc = jnp.exp(cum - ld_all)
    d_inv = jnp.exp(-cum)
    d_rem = jnp.exp(tot - cum)
    d_tot = jnp.exp(tot)
    kh_all = km_all * d_inv
    kt_all = km_all * d_rem
    kkraw = k_all * kk_ref[...]
    bonus_all = r_all * km_all * rk_ref[...]

    hs = lambda x: [x[:, h * n:(h + 1) * n] for h in heads]
    kk = [x / jnp.maximum(jnp.sqrt(jnp.sum(x * x, axis=-1, keepdims=True)), 1e-12) for x in hs(kkraw)]
    b = [x * a for x, a in zip(kk, hs(a_all))]
    v = hs(v_all)
    lhs = [jnp.concatenate([x * d, rd], axis=0) for x, d, rd in zip(kk, hs(d_exc), hs(rd_all))]
    rhs = [jnp.concatenate([x * d, kh], axis=0) for x, d, kh in zip(b, hs(d_inv), hs(kh_all))]
    gram = [_dot_nt(x, y) for x, y in zip(lhs, rhs)]
    l_b = [jnp.where(strict, g[:c, :c], 0.0) for g in gram]
    lak = [jnp.concatenate([jnp.where(strict, g[:c, c:], 0.0), jnp.where(incl, g[c:, c:], 0.0)], axis=0)
           for g in gram]
    a_b = [jnp.where(incl, g[c:, :c], 0.0) for g in gram]
    s0 = [s_scr[h] for h in heads]
    ks = [_dot_nt(x, s) for x, s in zip(lhs, s0)]
    lav = [_dot_hi(x, y) for x, y in zip(lak, v)]
    t_inv = _unit_lower_inverse(l_b, rowi, coli, c)
    u = [_dot_hi(t, -x[:c] - y[:c]) for t, x, y in zip(t_inv, ks, lav)]
    upd = [_dot_tn(jnp.concatenate([x, y], axis=0), jnp.concatenate([bb * d, kt], axis=0))
           for x, y, bb, d, kt in zip(u, v, b, hs(d_rem), hs(kt_all))]
    for h, (s, d, x) in enumerate(zip(s0, hs(d_tot), upd)):
        s_scr[h] = s * d + x
    y = [x[c:] + z[c:] + _dot_hi(ab, uu) for x, z, ab, uu in zip(ks, lav, a_b, u)]

    for h in heads:
        sl = slice(h * n, (h + 1) * n)
        mean = jnp.mean(y[h], axis=-1, keepdims=True)
        var = jnp.mean(jnp.square(y[h] - mean), axis=-1, keepdims=True)
        yn = (y[h] - mean) * lax.rsqrt(var + GN_EPS)
        yn = yn * lw_ref[:, sl] + lb_ref[:, sl]
        yn = yn + jnp.sum(bonus_all[:, sl], axis=-1, keepdims=True) * v[h]
        o_ref[:, sl] = (yn * g_all[:, sl]).astype(o_ref.dtype)

    @pl.when(ci == pl.num_programs(1) - 1)
    def _():
        sfin_ref[...] = s_scr[...]


def rwkv_branch(p_rw, prev_row, wkv0, mu, w0, w2, a0, a2, g2, k_k, k_a, r_k, lnx_w, lnx_b):
    bsz, t, _ = p_rw.shape
    c = RW_CHUNK
    assert t % c == 0
    row = lambda x: x.reshape(1, -1).astype(F32)
    vec = lambda n: pl.BlockSpec((1, n), lambda b, i: (0, 0))
    mat = lambda m, n: pl.BlockSpec((m, n), lambda b, i: (0, 0))
    state_spec = pl.BlockSpec((None, RW_HEADS, RW_HEAD_DIM, RW_HEAD_DIM), lambda b, i: (b, 0, 0, 0))
    return pl.pallas_call(
        _rwkv_kernel,
        name="rwkv",
        grid=(bsz, t // c),
        in_specs=[
            pl.BlockSpec((None, c, RW_COLS), lambda b, i: (b, i, 0)),
            pl.BlockSpec((None, 1, RW_COLS), lambda b, i: (b, 0, 0)),
            state_spec,
            vec(RW_COLS), vec(RW_WIDTH), mat(LORA_W, RW_WIDTH), vec(RW_WIDTH), mat(LORA_A, RW_WIDTH),
            mat(LORA_G, RW_WIDTH), vec(RW_WIDTH), vec(RW_WIDTH), vec(RW_WIDTH), vec(RW_WIDTH), vec(RW_WIDTH),
        ],
        out_specs=[
            pl.BlockSpec((None, c, RW_WIDTH), lambda b, i: (b, i, 0)),
            state_spec,
        ],
        out_shape=[
            jax.ShapeDtypeStruct((bsz, t, RW_WIDTH), BF16),
            jax.ShapeDtypeStruct((bsz, RW_HEADS, RW_HEAD_DIM, RW_HEAD_DIM), F32),
        ],
        scratch_shapes=[
            pltpu.VMEM((RW_HEADS, RW_HEAD_DIM, RW_HEAD_DIM), F32),
            pltpu.VMEM((1, RW_COLS), F32),
        ],
        compiler_params=_cparams(("arbitrary", "arbitrary")),
    )(p_rw, prev_row.reshape(bsz, 1, RW_COLS), wkv0.astype(F32), row(mu), row(w0), w2.astype(BF16), row(a0),
      a2.astype(BF16), g2.astype(BF16), row(k_k), row(k_a), row(r_k), row(lnx_w), row(lnx_b))


def _attn_kernel(sink_ref, q_ref, k0_ref, k1_ref, k2_ref, v0_ref, v1_ref, v2_ref, o_ref, *, first_valid):
    ci = pl.program_id(1)
    nk = (N_PREV_CHUNKS + 1) * CHUNK
    k3 = jnp.concatenate([k0_ref[...], k1_ref[...], k2_ref[...]], axis=0).astype(BF16)
    v3 = jnp.concatenate([v0_ref[...], v1_ref[...], v2_ref[...]], axis=0).astype(BF16)
    q = q_ref[...].astype(BF16)
    rows = ATT_GROUP * CHUNK
    qi = lax.broadcasted_iota(I32, (rows, nk), 0)
    kj = lax.broadcasted_iota(I32, (rows, nk), 1)
    dist = jnp.abs((qi % CHUNK) + N_PREV_CHUNKS * CHUNK - kj).astype(F32)
    valid = (kj // CHUNK + ci) >= first_valid
    scale = ATT_HEAD_DIM ** -0.5
    for g in range(ATT_KV_HEADS):
        kg = k3[:, g * ATT_HEAD_DIM:(g + 1) * ATT_HEAD_DIM]
        vg = v3[:, g * ATT_HEAD_DIM:(g + 1) * ATT_HEAD_DIM]
        heads = [g * ATT_GROUP + j for j in range(ATT_GROUP)]
        qg = jnp.concatenate([q[:, h * ATT_HEAD_DIM:(h + 1) * ATT_HEAD_DIM] for h in heads], axis=0)
        s = lax.dot_general(qg, kg, (((1,), (1,)), ((), ())), preferred_element_type=F32) * scale
        slope = jnp.concatenate(
            [jnp.full((CHUNK, 1), 2.0 ** (-8.0 * (h + 1) / ATT_Q_HEADS), F32) for h in heads], axis=0)
        sink = jnp.concatenate([jnp.full((CHUNK, 1), sink_ref[h], F32) for h in heads], axis=0)
        s = s - slope * dist
        s = jnp.where(valid, s, -1e30)
        m = jnp.maximum(jnp.max(s, axis=-1, keepdims=True), sink)
        p = jnp.exp(s - m)
        denom = jnp.sum(p, axis=-1, keepdims=True) + jnp.exp(sink - m)
        o = jnp.dot((p / denom).astype(BF16), vg, preferred_element_type=F32)
        for j, h in enumerate(heads):
            o_ref[:, h * ATT_HEAD_DIM:(h + 1) * ATT_HEAD_DIM] = o[j * CHUNK:(j + 1) * CHUNK].astype(o_ref.dtype)


def attn_branch(q, k_band, v_band, sinks, first_valid):
    bsz, t, _ = q.shape
    nc = t // CHUNK
    kv_spec = lambda j: pl.BlockSpec((None, CHUNK, ATT_KV_WIDTH), lambda b, c: (b, c + j, 0))
    return pl.pallas_call(
        functools.partial(_attn_kernel, first_valid=first_valid),
        name="attn",
        grid=(bsz, nc),
        in_specs=[
            pl.BlockSpec(memory_space=pltpu.SMEM),
            pl.BlockSpec((None, CHUNK, ATT_Q_WIDTH), lambda b, c: (b, c, 0)),
            kv_spec(0), kv_spec(1), kv_spec(2), kv_spec(0), kv_spec(1), kv_spec(2),
        ],
        out_specs=pl.BlockSpec((None, CHUNK, ATT_Q_WIDTH), lambda b, c: (b, c, 0)),
        out_shape=jax.ShapeDtypeStruct((bsz, t, ATT_Q_WIDTH), BF16),
        compiler_params=_cparams(("arbitrary", "arbitrary")),
    )(sinks.astype(F32), q, k_band, k_band, k_band, v_band, v_band, v_band)


def _top_rows(s, k):
    nrows = s.shape[0]
    row = lax.broadcasted_iota(I32, s.shape, 0)
    vals, idxs = [], []
    for _ in range(k):
        m = jnp.max(s, axis=0, keepdims=True)
        i = jnp.min(jnp.where(s == m, row, nrows), axis=0, keepdims=True)
        vals.append(m)
        idxs.append(i)
        s = jnp.where(row == i, -jnp.inf, s)
    return jnp.concatenate(vals, axis=0), jnp.concatenate(idxs, axis=0)


def _pick_rows(table, sel):
    r = table.shape[0]
    row = lax.broadcasted_iota(I32, table.shape, 0)
    out = []
    for j in range(sel.shape[0]):
        out.append(jnp.sum(jnp.where(row == sel[j:j + 1, :], table, 0), axis=0, keepdims=True))
    return jnp.concatenate(out, axis=0)


def _merge_kernel(x_ref, orw_ref, oatt_ref, pg_ref, wb_ref, wo_ref, g2_ref, wq_ref, sk_ref,
                  x1_ref, h2_ref, idx_ref, gate_ref, h2b_scr):
    d = x_ref.shape[1]

    @pl.when(pl.program_id(1) == 0)
    def _():
        gates = jax.nn.sigmoid(pg_ref[...])
        merged = (gates[:, :d] * jnp.dot(orw_ref[...], wb_ref[0], preferred_element_type=F32)
                  + gates[:, d:] * jnp.dot(oatt_ref[...], wb_ref[1], preferred_element_type=F32))
        x1 = x_ref[...] + jnp.dot(merged.astype(BF16), wo_ref[...], preferred_element_type=F32)
        x1_ref[...] = x1
        h2 = _rms(x1, g2_ref[...])
        h2_ref[...] = h2
        h2b_scr[...] = h2.astype(BF16)

    qv = jnp.dot(h2b_scr[...], wq_ref[...], preferred_element_type=F32).astype(BF16)
    top = []
    for p in range(2):
        s = lax.dot_general(sk_ref[p], qv[:, p * PEER_HALF:(p + 1) * PEER_HALF], (((1,), (1,)), ((), ())),
                            preferred_element_type=F32)
        top.append(_top_rows(s, PEER_TOPK))
    (s1, i1), (s2, i2) = top
    cand = jnp.concatenate([s1[i:i + 1, :] + s2 for i in range(PEER_TOPK)], axis=0)
    best, pos = _top_rows(cand, PEER_TOPK)
    e1 = _pick_rows(i1, pos // PEER_TOPK)
    e2 = _pick_rows(i2, pos % PEER_TOPK)
    ex = jnp.exp(best - best[0:1, :])
    idx_ref[...] = e1 * PEER_NKEYS + e2
    gate_ref[...] = ex / jnp.sum(ex, axis=0, keepdims=True)


def merge_and_route(x2, o_rw, o_att, p_gate, wb_bf, wo_bf, g2, wq_bf, sk_bf):
    n, d = x2.shape
    tm = _row_tile(n, 256)
    full = lambda shape: pl.BlockSpec(shape, lambda i, h: (0,) * len(shape))
    rows = lambda w: pl.BlockSpec((tm, w), lambda i, h: (i, 0))
    head_w = 2 * PEER_HALF
    return pl.pallas_call(
        _merge_kernel,
        name="merge_route",
        grid=(n // tm, PEER_HEADS),
        in_specs=[
            rows(d), rows(RW_WIDTH), rows(ATT_Q_WIDTH), rows(2 * d),
            full(wb_bf.shape), full(wo_bf.shape), full((1, d)),
            pl.BlockSpec((d, head_w), lambda i, h: (0, h)),
            full(sk_bf.shape),
        ],
        out_specs=[
            rows(d), rows(d),
            pl.BlockSpec((PEER_TOPK, tm), lambda i, h: (h, i)),
            pl.BlockSpec((PEER_TOPK, tm), lambda i, h: (h, i)),
        ],
        out_shape=[
            jax.ShapeDtypeStruct((n, d), F32),
            jax.ShapeDtypeStruct((n, d), F32),
            jax.ShapeDtypeStruct((PEER_SEL, n), I32),
            jax.ShapeDtypeStruct((PEER_SEL, n), F32),
        ],
        scratch_shapes=[pltpu.VMEM((tm, d), BF16)],
        compiler_params=_cparams(("arbitrary", "arbitrary")),
    )(x2, o_rw, o_att, p_gate, wb_bf, wo_bf, g2.reshape(1, d), wq_bf, sk_bf)


def _expert_kernel(idx_ref, gate_ref, h2_ref, x1_ref, fg_ref, tab_ref, tabrows_ref, y_ref, *scratch):
    bufs, sem = scratch[:GATHER_SLOTS], scratch[GATHER_SLOTS]
    tb, d = h2_ref.shape
    lanes = 128
    sub = d // lanes

    def start_token(n, slot):
        for m in range(PEER_SEL):
            pltpu.make_async_copy(tab_ref.at[idx_ref[n, m]], bufs[slot].at[pl.ds(m * sub, sub), :],
                                  sem.at[slot]).start(priority=m % 2)

    def wait_token(slot):
        pltpu.make_async_copy(tabrows_ref.at[pl.ds(0, PEER_SEL * sub), :], bufs[slot], sem.at[slot]).wait()

    for slot in range(GATHER_SLOTS - 1):
        start_token(slot, slot)

    eye = lax.broadcasted_iota(I32, (PEER_SEL, PEER_SEL), 0) == lax.broadcasted_iota(I32, (PEER_SEL, PEER_SEL), 1)
    hi_mask = jnp.uint32(0xFFFF0000)

    def token(n, slot):
        hrow = h2_ref[pl.ds(n, 1), :]
        acc = jnp.zeros((PEER_SEL, lanes), F32)
        for s in range(sub):
            w = bufs[slot][pl.ds(s, PEER_SEL, stride=sub), :]
            u = lax.bitcast_convert_type(w & hi_mask, F32)
            acc = acc + u * hrow[:, s * lanes:(s + 1) * lanes]
        z = jnp.sum(acc, axis=-1, keepdims=True)
        zrow = jnp.sum(jnp.where(eye, z, 0.0), axis=0, keepdims=True)
        arow = 0.5 * zrow * (1.0 + lax.erf(zrow * (2.0 ** -0.5))) * gate_ref[pl.ds(n, 1), :]
        act = jnp.sum(jnp.where(eye, arow, 0.0), axis=-1, keepdims=True)
        outs = []
        for s in range(sub):
            w = bufs[slot][pl.ds(s, PEER_SEL, stride=sub), :]
            v = lax.bitcast_convert_type(w << 16, F32)
            outs.append(jnp.sum(v * act, axis=0, keepdims=True))
        x2 = x1_ref[pl.ds(n, 1), :] + jnp.concatenate(outs, axis=1)
        y_ref[pl.ds(n, 1), :] = _rms(x2, fg_ref[...])

    n_groups = tb // GATHER_SLOTS

    def group(g, last):
        for slot in range(GATHER_SLOTS):
            n = g * GATHER_SLOTS + slot
            wait_token(slot)
            if not last or slot == 0:
                start_token(n + GATHER_SLOTS - 1, (slot + GATHER_SLOTS - 1) % GATHER_SLOTS)
            token(n, slot)

    def body(g, carry):
        group(g, False)
        return carry

    lax.fori_loop(0, n_groups - 1, body, 0)
    group(n_groups - 1, True)


def peer_experts(idx, gate, h2, x1, final_g, table):
    n, d = h2.shape
    e = table.shape[0]
    lanes = 128
    tb = _row_tile(n, 128)
    assert tb % GATHER_SLOTS == 0 and d % lanes == 0
    rows = lambda w: pl.BlockSpec((tb, w), lambda i: (i, 0))
    return pl.pallas_call(
        _expert_kernel,
        name="experts",
        grid=(n // tb,),
        in_specs=[
            pl.BlockSpec((tb, PEER_SEL), lambda i: (i, 0), memory_space=pltpu.SMEM),
            rows(PEER_SEL), rows(d), rows(d),
            pl.BlockSpec((1, d), lambda i: (0, 0)),
            pl.BlockSpec(memory_space=pl.ANY),
            pl.BlockSpec(memory_space=pl.ANY),
        ],
        out_specs=rows(d),
        out_shape=jax.ShapeDtypeStruct((n, d), F32),
        scratch_shapes=[pltpu.VMEM((PEER_SEL * (d // lanes), lanes), U32) for _ in range(GATHER_SLOTS)]
        + [pltpu.SemaphoreType.DMA((GATHER_SLOTS,))],
        compiler_params=_cparams(("arbitrary",)),
    )(idx, gate, h2, x1, final_g.reshape(1, d), table.reshape(e, d // lanes, lanes),
      table.reshape(e * (d // lanes), lanes))


def pack_expert_table(u, v):
    ub = lax.bitcast_convert_type(u.astype(BF16), jnp.uint16).astype(U32)
    vb = lax.bitcast_convert_type(v.astype(BF16), jnp.uint16).astype(U32)
    return (ub << 16) | vb


def _trunk(x, prev_row, wkv0, k_band_prefix, v_band_prefix, first_valid, lp):
    bsz, t, d = x.shape
    n = bsz * t
    x2 = x.reshape(n, d)
    p_rw, p_att, p_gate = in_proj(x2, lp["norm1_g"], lp["w_in"])
    h_last = rms_rows(x[:, -1, :], lp["norm1_g"])
    o_rw, wkv_new = rwkv_branch(p_rw.reshape(bsz, t, RW_COLS), prev_row, wkv0, lp["mu_shift"], lp["w_decay0"],
                                lp["w_decay_lora"], lp["a_icl0"], lp["a_icl_lora"], lp["g_lora"], lp["k_k"],
                                lp["k_a"], lp["r_k"], lp["lnx_w"], lp["lnx_b"])
    p_att3 = p_att.reshape(bsz, t, ATT_COLS)
    q = p_att3[..., :ATT_Q_WIDTH]
    k = p_att3[..., ATT_Q_WIDTH:ATT_Q_WIDTH + ATT_KV_WIDTH]
    v = p_att3[..., ATT_Q_WIDTH + ATT_KV_WIDTH:]
    k_band = jnp.concatenate([k_band_prefix, k], axis=1)
    v_band = jnp.concatenate([v_band_prefix, v], axis=1)
    o_att = attn_branch(q, k_band, v_band, lp["attn_sinks"], first_valid)
    x1, h2, idx_t, gate_t = merge_and_route(x2, o_rw.reshape(n, RW_WIDTH), o_att.reshape(n, ATT_Q_WIDTH), p_gate,
                                            lp["w_branch"], lp["w_out"], lp["norm2_g"], lp["peer_wq"],
                                            lp["peer_sub_keys"])
    y = peer_experts(idx_t.T, gate_t.T, h2, x1, lp["final_g"], lp["table"])
    kv_shape = (bsz, -1, ATT_KV_HEADS, ATT_HEAD_DIM)
    return y.reshape(bsz, t, d), h_last, wkv_new, k.reshape(kv_shape), v.reshape(kv_shape)


def kernel(x_prompt, x_sample, state_shift, state_wkv, cache_k, cache_v, norm1_g, w_in, mu_shift, w_decay0, w_decay_lora, a_icl0, a_icl_lora, g_lora, k_k, k_a, r_k, lnx_w, lnx_b, attn_sinks, w_branch, w_out, norm2_g, peer_wq, peer_sub_keys, peer_u, peer_v, final_g):
    assert w_in.shape[0] == 1, "single-layer trunk"
    lp = dict(
        norm1_g=norm1_g[0], w_in=w_in[0].astype(BF16), mu_shift=mu_shift[0], w_decay0=w_decay0[0],
        w_decay_lora=w_decay_lora[0], a_icl0=a_icl0[0], a_icl_lora=a_icl_lora[0], g_lora=g_lora[0],
        k_k=k_k[0], k_a=k_a[0], r_k=r_k[0].reshape(-1), lnx_w=lnx_w[0], lnx_b=lnx_b[0],
        attn_sinks=attn_sinks[0], w_branch=w_branch[0].astype(BF16), w_out=w_out[0].astype(BF16),
        norm2_g=norm2_g[0], peer_wq=peer_wq[0].astype(BF16), peer_sub_keys=peer_sub_keys[0].astype(BF16),
        final_g=final_g, table=pack_expert_table(peer_u[0], peer_v[0]),
    )
    bp, tp, _ = x_prompt.shape
    bs, ts, _ = x_sample.shape
    keep = min(WINDOW, tp)

    zeros_band = jnp.zeros((bp, WINDOW, ATT_KV_WIDTH), F32)
    yp, shift_p, wkv_p, kp, vp = _trunk(
        x_prompt, jnp.zeros((bp, RW_COLS), F32), jnp.zeros((bp, RW_HEADS, RW_HEAD_DIM, RW_HEAD_DIM), F32),
        zeros_band, zeros_band, N_PREV_CHUNKS, lp)

    prev_row = shift_proj(state_shift[0], lp["w_in"][:, :RW_COLS])
    ys, shift_s, wkv_s, ks, vs = _trunk(
        x_sample, prev_row, state_wkv[0], cache_k[0].reshape(bs, -1, ATT_KV_WIDTH),
        cache_v[0].reshape(bs, -1, ATT_KV_WIDTH), 0, lp)

    return (yp, ys, shift_p[None], wkv_p[None], kp[:, tp - keep:][None], vp[:, tp - keep:][None],
            shift_s[None], wkv_s[None], ks[None], vs[None])
```
